```python
import jax, jax.numpy as jnp
from jax import lax
import numpy as np


D_MODEL = 1024
BATCH = 8
SEQ = 2048
DEPTH = 2

CHUNK = 64
PLE_DIM = 256
EPS = 1e-6

HG_HEADS = 4
HG_DK = 128
HG_DV = 128
HG_WIDTH = HG_HEADS * HG_DK

CONV_WIDTH = 512
CONV_K = 31

D_FF = 2816
FFN_CONV_K = 3

COL_Q = HG_WIDTH
COL_F = HG_WIDTH
COL_I = HG_HEADS * HG_DV
COL_OG = HG_HEADS * HG_DV
COL_GLU = 2 * CONV_WIDTH
COL_GATE = 2 * D_MODEL
SPLITS = tuple(int(v) for v in np.cumsum([COL_Q, COL_F, COL_I, COL_OG, COL_GLU, D_MODEL]))
IN_COLS = COL_Q + COL_F + COL_I + COL_OG + COL_GLU + COL_GATE

kernel_name = "hybrid_hgrn2_conformer_convffn_ple"


def rmsnorm(x, g):
    xf = x.astype(jnp.float32)
    y = xf * lax.rsqrt(jnp.mean(xf * xf, axis=-1, keepdims=True) + EPS)
    return (y * g.astype(jnp.float32)).astype(x.dtype)


def layernorm(x, g, b):
    xf = x.astype(jnp.float32)
    mu = jnp.mean(xf, axis=-1, keepdims=True)
    var = jnp.mean(jnp.square(xf - mu), axis=-1, keepdims=True)
    y = (xf - mu) * lax.rsqrt(var + EPS)
    return (y * g.astype(jnp.float32) + b.astype(jnp.float32)).astype(x.dtype)


def causal_dwconv(x, w, b):
    k, c = w.shape
    y = lax.conv_general_dilated(
        x, w[:, None, :].astype(x.dtype), window_strides=(1,), padding=[(k - 1, 0)],
        dimension_numbers=("NWC", "WIO", "NWC"), feature_group_count=c)
    return y + b.astype(x.dtype)


def hgrn2_recurrence(q, k, v, log_f):
    bsz, seq, h, dk = q.shape
    dv = v.shape[-1]
    nc = seq // CHUNK

    def to_chunks(t):
        return t.astype(jnp.float32).reshape(bsz, nc, CHUNK, h, t.shape[-1]).transpose(1, 0, 3, 2, 4)

    qc, kc, vc, gc = to_chunks(q), to_chunks(k), to_chunks(v), to_chunks(log_f)
    mask = jnp.tril(jnp.ones((CHUNK, CHUNK), dtype=bool))[:, :, None]

    def step(state, inp):
        qb, kb, vb, gb = inp
        cum = jnp.cumsum(gb, axis=-2)
        o_inter = jnp.einsum("bhtk,bhkv->bhtv", qb * jnp.exp(cum), state)
        diff = cum[:, :, :, None, :] - cum[:, :, None, :, :]
        decay = jnp.exp(jnp.where(mask, diff, -jnp.inf))
        scores = jnp.einsum("bhtk,bhsk,bhtsk->bhts", qb, kb, decay)
        o_intra = jnp.einsum("bhts,bhsv->bhtv", scores, vb)
        last = cum[:, :, -1:, :]
        new_state = (jnp.exp(last[:, :, 0, :])[..., None] * state
                     + jnp.einsum("bhsk,bhsv->bhkv", kb * jnp.exp(last - cum), vb))
        return new_state, o_inter + o_intra

    s0 = jnp.zeros((bsz, h, dk, dv), jnp.float32)
    _, out = lax.scan(step, s0, (qc, kc, vc, gc))
    return out.transpose(1, 0, 3, 2, 4).reshape(bsz, seq, h, dv)


def setup_inputs(seed: int = 0) -> dict:
    key = jax.random.key(seed)
    ks = iter(jax.random.split(key, 32))
    nrm = lambda shape, scale: jax.random.normal(next(ks), shape, jnp.float32) * scale
    gain = lambda shape: 1.0 + nrm(shape, 0.01)
    return {
        "x": nrm((BATCH, SEQ, D_MODEL), 1.0),
        "p": nrm((DEPTH, BATCH, SEQ, PLE_DIM), 1.0),
        "g_mix": gain((DEPTH, D_MODEL)),
        "w_in": nrm((DEPTH, D_MODEL, IN_COLS), D_MODEL ** -0.5),
        "hg_lb_logits": nrm((DEPTH, HG_WIDTH), 0.5),
        "hg_norm_g": gain((DEPTH, HG_DV)),
        "w_br_a": nrm((DEPTH, HG_HEADS * HG_DV, D_MODEL), (HG_HEADS * HG_DV) ** -0.5),
        "b_glu": nrm((DEPTH, COL_GLU), 0.01),
        "conv_w": nrm((DEPTH, CONV_K, CONV_WIDTH), CONV_K ** -0.5),
        "conv_b": nrm((DEPTH, CONV_WIDTH), 0.01),
        "ln_g": gain((DEPTH, CONV_WIDTH)),
        "ln_b": nrm((DEPTH, CONV_WIDTH), 0.01),
        "w_br_b": nrm((DEPTH, CONV_WIDTH, D_MODEL), CONV_WIDTH ** -0.5),
        "w_out": nrm((DEPTH, D_MODEL, D_MODEL), D_MODEL ** -0.5),
        "g_ffn": gain((DEPTH, D_MODEL)),
        "w_up": nrm((DEPTH, D_MODEL, 2 * D_FF), D_MODEL ** -0.5),
        "ffn_conv_w": nrm((DEPTH, FFN_CONV_K, 2 * D_FF), FFN_CONV_K ** -0.5),
        "ffn_conv_b": nrm((DEPTH, 2 * D_FF), 0.01),
        "w_down": nrm((DEPTH, D_FF, D_MODEL), D_FF ** -0.5),
        "g_ple": gain((DEPTH, D_MODEL)),
        "w_ple_gate": nrm((DEPTH, D_MODEL, D_MODEL), D_MODEL ** -0.5),
        "w_ple_proj": nrm((DEPTH, PLE_DIM, D_MODEL), PLE_DIM ** -0.5),
        "g_final": gain((D_MODEL,)),
    }


def reference(x, p, g_mix, w_in, hg_lb_logits, hg_norm_g, w_br_a, b_glu, conv_w, conv_b,
              ln_g, ln_b, w_br_b, w_out, g_ffn, w_up, ffn_conv_w, ffn_conv_b, w_down,
              g_ple, w_ple_gate, w_ple_proj, g_final):
    bsz, seq, _ = x.shape
    lb_all = jnp.cumsum(jax.nn.softmax(hg_lb_logits.astype(jnp.float32), axis=0), axis=0)
    lb_all = lb_all - lb_all[0:1]

    for i in range(DEPTH):
        h = rmsnorm(x, g_mix[i])
        proj = h @ w_in[i]
        zq, zf, zi, zog, zglu, zga, zgb = jnp.split(proj, SPLITS, axis=-1)

        lb = lb_all[i]
        log_f = jnp.logaddexp(jnp.log(lb), jnp.log1p(-lb) + jax.nn.log_sigmoid(zf.astype(jnp.float32)))
        k_in = -jnp.expm1(log_f)
        q = jax.nn.silu(zq.astype(jnp.float32))
        o = hgrn2_recurrence(q.reshape(bsz, seq, HG_HEADS, HG_DK),
                             k_in.reshape(bsz, seq, HG_HEADS, HG_DK),
                             zi.reshape(bsz, seq, HG_HEADS, HG_DV),
                             log_f.reshape(bsz, seq, HG_HEADS, HG_DK))
        o = rmsnorm(o, hg_norm_g[i]).reshape(bsz, seq, HG_HEADS * HG_DV)
        o = (o * jax.nn.silu(zog.astype(jnp.float32))).astype(x.dtype)
        y_a = o @ w_br_a[i]

        u = zglu + b_glu[i]
        u = u[..., :CONV_WIDTH] * jax.nn.sigmoid(u[..., CONV_WIDTH:])
        u = causal_dwconv(u, conv_w[i], conv_b[i])
        u = jax.nn.silu(layernorm(u, ln_g[i], ln_b[i]))
        y_b = u @ w_br_b[i]

        y = jax.nn.sigmoid(zga) * y_a + jax.nn.sigmoid(zgb) * y_b
        x = x + y @ w_out[i]

        hf = rmsnorm(x, g_ffn[i])
        up = causal_dwconv(hf @ w_up[i], ffn_conv_w[i], ffn_conv_b[i])
        x = x + (jax.nn.silu(up[..., :D_FF]) * up[..., D_FF:]) @ w_down[i]

        gate = jax.nn.sigmoid(rmsnorm(x, g_ple[i]) @ w_ple_gate[i])
        x = x + gate * (p[i] @ w_ple_proj[i])

    return rmsnorm(x, g_final)
```

```python
import functools

import jax
import jax.numpy as jnp
from jax import lax
from jax.experimental import pallas as pl
from jax.experimental.pallas import tpu as pltpu

D_MODEL = 1024
DEPTH = 2
CHUNK = 64
PLE_DIM = 256
EPS = 1e-6
HG_HEADS = 4
HG_DK = 128
HG_DV = 128
HG_WIDTH = HG_HEADS * HG_DK
CONV_WIDTH = 512
CONV_K = 31
D_FF = 2816
FFN_CONV_K = 3
IN_COLS = 4 * HG_WIDTH + 2 * CONV_WIDTH + 2 * D_MODEL

OFF_Q = 0
OFF_F = HG_WIDTH
OFF_I = 2 * HG_WIDTH
OFF_OG = 3 * HG_WIDTH
OFF_GLU = 4 * HG_WIDTH
OFF_GA = OFF_GLU + 2 * CONV_WIDTH
OFF_GB = OFF_GA + D_MODEL

TILE_T = 256
CONV_HALO = 32
FFN_HALO = 8
FFN_COLS = 256
LEVELS = (32, 16, 8, 4, 2, 1)
VMEM_LIMIT_BYTES = 58 * 1024 * 1024

F32 = jnp.float32
BF16 = jnp.bfloat16


def _dot(a, b):
    return jnp.dot(a, b, preferred_element_type=F32)


def _dot_nt(a, b):
    return lax.dot_general(a, b, (((1,), (1,)), ((), ())), preferred_element_type=F32)


def _dot_tn(a, b):
    return lax.dot_general(a, b, (((0,), (0,)), ((), ())), preferred_element_type=F32)


def _rms(x, g):
    return x * lax.rsqrt(jnp.mean(x * x, axis=-1, keepdims=True) + EPS) * g


def _sigmoid(x):
    return jax.nn.sigmoid(x)


def _silu(x):
    return x * jax.nn.sigmoid(x)


def _log_sigmoid(z):
    return jnp.minimum(z, 0.0) - jnp.log1p(jnp.exp(-jnp.abs(z)))


def _split3(g):
    hi = g.astype(BF16)
    r1 = g - hi.astype(F32)
    mid = r1.astype(BF16)
    lo = (r1 - mid.astype(F32)).astype(BF16)
    return hi, mid, lo


def _level_ref(cum3, half):
    n8, _, w = cum3.shape
    if half >= 8:
        g = 2 * half // 8
        c4 = cum3.reshape(n8 // g, g, 8, w)
        r = jnp.broadcast_to(c4[:, g // 2 - 1:g // 2, 7:8, :], c4.shape)
        return r.reshape(cum3.shape)
    sub = lax.broadcasted_iota(jnp.int32, cum3.shape, 1)
    rows = [jnp.broadcast_to(cum3[:, s + half - 1:s + half, :], cum3.shape)
            for s in range(0, 8, 2 * half)]
    r = rows[-1]
    for i in range(len(rows) - 2, -1, -1):
        r = jnp.where(sub < (i + 1) * 2 * half, rows[i], r)
    return r


def _hgrn_chunk(r0, zq_ref, zf_ref, zi_ref, zog_ref, log_lb, log1m_lb, one_m_lb, hg_g,
                state_ref, oa_ref):
    rows = pl.ds(r0, CHUNK)
    zq = zq_ref[rows, :]
    zf = zf_ref[rows, :]
    v = zi_ref[rows, :]
    zog = zog_ref[rows, :]

    b = log1m_lb + _log_sigmoid(zf)
    log_f = jnp.maximum(log_lb, b) + jnp.log1p(jnp.exp(-jnp.abs(log_lb - b)))
    kk = one_m_lb * _sigmoid(-zf)
    q = _silu(zq)

    ti = lax.broadcasted_iota(jnp.int32, (CHUNK, CHUNK), 0)
    si = lax.broadcasted_iota(jnp.int32, (CHUNK, CHUNK), 1)
    tril = (si <= ti).astype(BF16)
    hi, mid, lo = _split3(log_f)
    cum = _dot(tril, hi) + _dot(tril, mid) + _dot(tril, lo)

    cum3 = cum.reshape(CHUNK // 8, 8, HG_WIDTH)
    last = jnp.broadcast_to(cum3[CHUNK // 8 - 1:, 7:8, :], cum3.shape).reshape(cum.shape)
    q_dec = (q * jnp.exp(cum)).astype(BF16)
    k_dec = (kk * jnp.exp(last - cum)).astype(BF16)
    v_bf = v.astype(BF16)
    qk = q * kk

    pos = lax.broadcasted_iota(jnp.int32, cum.shape, 0)
    q_lv, k_lv = [], []
    for half in LEVELS:
        ref = _level_ref(cum3, half).reshape(cum.shape)
        upper = (pos & half) != 0
        e = jnp.exp(jnp.where(upper, cum - ref, ref - cum))
        q_lv.append(jnp.where(upper, q * e, 0.0).astype(BF16))
        k_lv.append(jnp.where(upper, 0.0, kk * e).astype(BF16))

    txs = ti ^ si
    for hh in range(HG_HEADS):
        lanes = slice(hh * HG_DK, (hh + 1) * HG_DK)
        scores = jnp.zeros((CHUNK, CHUNK), F32)
        for li, half in enumerate(LEVELS):
            s_l = _dot_nt(q_lv[li][:, lanes], k_lv[li][:, lanes])
            scores = scores + jnp.where(txs < 2 * half, s_l, 0.0)
        st = state_ref[hh]
        o = _dot(scores.astype(BF16), v_bf[:, lanes])
        o = o + _dot_nt(q_dec[:, lanes], st.astype(BF16))
        o = o + jnp.sum(qk[:, lanes], axis=-1, keepdims=True) * v[:, lanes]
        decay = jnp.exp(last[0:1, lanes])
        state_ref[hh] = st * decay + _dot_tn(v_bf[:, lanes], k_dec[:, lanes])
        on = _rms(o, hg_g) * _silu(zog[:, lanes])
        oa_ref[rows, lanes] = on.astype(BF16)


def _layer_kernel(layer, final_norm,
                  x_ref, p_ref, lbl_ref, g_mix_ref, w_in_ref, hg_g_ref, w_bra_ref, b_glu_ref,
                  cw_ref, cb_ref, ln_g_ref, ln_b_ref, w_brb_ref, w_out_ref, g_ffn_ref,
                  w_up_ref, fcw_ref, fcb_ref, w_down_ref, g_ple_ref, w_pg_ref, w_pp_ref,
                  g_fin_ref, o_ref,
                  state_ref, ubuf_ref, fhalo_ref, act_ref, oa_ref,
                  zq_ref, zf_ref, zi_ref, zog_ref):
    tt = TILE_T

    @pl.when(pl.program_id(1) == 0)
    def _():
        state_ref[...] = jnp.zeros_like(state_ref)
        ubuf_ref[0:CONV_HALO, :] = jnp.zeros((CONV_HALO, CONV_WIDTH), F32)
        fhalo_ref[...] = jnp.zeros_like(fhalo_ref)

    x = x_ref[0]
    h = _rms(x, g_mix_ref[...]).astype(BF16)

    lg = lbl_ref[...]
    ex = jnp.exp(lg - jnp.max(lg, axis=0, keepdims=True))
    sm = ex / jnp.sum(ex, axis=0, keepdims=True)
    cs0 = sm[0:1, :]
    cs = cs0
    for j in range(1, layer + 1):
        cs = cs + sm[j:j + 1, :]
    lb = cs - cs0
    log_lb = jnp.log(lb)
    log1m_lb = jnp.log1p(-lb)
    one_m_lb = 1.0 - lb

    zq_ref[...] = _dot(h, w_in_ref[:, OFF_Q:OFF_Q + HG_WIDTH])
    zf_ref[...] = _dot(h, w_in_ref[:, OFF_F:OFF_F + HG_WIDTH])
    zi_ref[...] = _dot(h, w_in_ref[:, OFF_I:OFF_I + HG_WIDTH])
    zog_ref[...] = _dot(h, w_in_ref[:, OFF_OG:OFF_OG + HG_WIDTH])
    hg_g = hg_g_ref[...]
    for c in range(tt // CHUNK):
        _hgrn_chunk(c * CHUNK, zq_ref, zf_ref, zi_ref, zog_ref, log_lb, log1m_lb, one_m_lb,
                    hg_g, state_ref, oa_ref)
    y_a = _dot(oa_ref[...], w_bra_ref[...])

    zglu = _dot(h, w_in_ref[:, OFF_GLU:OFF_GA]) + b_glu_ref[...]
    u = zglu[:, :CONV_WIDTH] * _sigmoid(zglu[:, CONV_WIDTH:])
    ubuf_ref[CONV_HALO:CONV_HALO + tt, :] = u
    acc = jnp.broadcast_to(cb_ref[...], (tt, CONV_WIDTH))
    for j in range(CONV_K):
        start = CONV_HALO - (CONV_K - 1) + j
        acc = acc + cw_ref[j:j + 1, :] * ubuf_ref[start:start + tt, :]
    ubuf_ref[0:CONV_HALO, :] = ubuf_ref[tt:tt + CONV_HALO, :]
    mu = jnp.mean(acc, axis=-1, keepdims=True)
    d = acc - mu
    var = jnp.mean(d * d, axis=-1, keepdims=True)
    ub = _silu(d * lax.rsqrt(var + EPS) * ln_g_ref[...] + ln_b_ref[...])
    y_b = _dot(ub.astype(BF16), w_brb_ref[...])

    zga = _dot(h, w_in_ref[:, OFF_GA:OFF_GB])
    zgb = _dot(h, w_in_ref[:, OFF_GB:OFF_GB + D_MODEL])
    y = _sigmoid(zga) * y_a + _sigmoid(zgb) * y_b
    x = x + _dot(y.astype(BF16), w_out_ref[...])

    hf = _rms(x, g_ffn_ref[...]).astype(BF16)
    rowid = lax.broadcasted_iota(jnp.int32, (tt, FFN_COLS), 0)

    def ffn_conv(col0):
        cols = slice(col0, col0 + FFN_COLS)
        up = _dot(hf, w_up_ref[:, cols])
        halo = fhalo_ref[:, cols]
        fhalo_ref[:, cols] = up[tt - FFN_HALO:, :]
        h1 = halo[FFN_HALO - 1:FFN_HALO, :]
        h2 = halo[FFN_HALO - 2:FFN_HALO - 1, :]
        prev1 = jnp.where(rowid == 0, h1, pltpu.roll(up, 1, 0))
        prev2 = jnp.where(rowid == 0, h2, jnp.where(rowid == 1, h1, pltpu.roll(up, 2, 0)))
        return (fcw_ref[0:1, cols] * prev2 + fcw_ref[1:2, cols] * prev1
                + fcw_ref[2:3, cols] * up + fcb_ref[:, cols])

    for j in range(D_FF // FFN_COLS):
        ca = ffn_conv(j * FFN_COLS)
        cg = ffn_conv(D_FF + j * FFN_COLS)
        act_ref[:, j * FFN_COLS:(j + 1) * FFN_COLS] = (_silu(ca) * cg).astype(BF16)
    x = x + _dot(act_ref[...], w_down_ref[...])

    hp = _rms(x, g_ple_ref[...]).astype(BF16)
    gate = _sigmoid(_dot(hp, w_pg_ref[...]))
    x = x + gate * _dot(p_ref[0, 0].astype(BF16), w_pp_ref[...])

    if final_norm:
        x = _rms(x, g_fin_ref[...])
    o_ref[0] = x


def _resident(shape):
    nd = len(shape)
    return pl.BlockSpec(shape, lambda b, t: (0,) * nd, pipeline_mode=pl.Buffered(1))


def _layer_call(layer, final_norm, x, p, small, weights):
    bsz, seq, _ = x.shape
    tt = TILE_T
    assert seq % tt == 0 and tt % CHUNK == 0
    (lbl, g_mix, hg_g, b_glu, cw, cb, ln_g, ln_b, g_ffn, fcw, fcb, g_ple, g_fin) = small
    (w_in, w_bra, w_brb, w_out, w_up, w_down, w_pg, w_pp) = weights
    operands = (x, p, lbl, g_mix, w_in, hg_g, w_bra, b_glu, cw, cb, ln_g, ln_b, w_brb, w_out,
                g_ffn, w_up, fcw, fcb, w_down, g_ple, w_pg, w_pp, g_fin)
    in_specs = [
        pl.BlockSpec((1, tt, D_MODEL), lambda b, t: (b, t, 0)),
        pl.BlockSpec((1, 1, tt, PLE_DIM), lambda b, t: (layer, b, t, 0)),
    ] + [_resident(a.shape) for a in operands[2:]]
    scratch = [
        pltpu.VMEM((HG_HEADS, HG_DV, HG_DK), F32),
        pltpu.VMEM((CONV_HALO + tt, CONV_WIDTH), F32),
        pltpu.VMEM((FFN_HALO, 2 * D_FF), F32),
        pltpu.VMEM((tt, D_FF), BF16),
        pltpu.VMEM((tt, HG_WIDTH), BF16),
        pltpu.VMEM((tt, HG_WIDTH), F32),
        pltpu.VMEM((tt, HG_WIDTH), F32),
        pltpu.VMEM((tt, HG_WIDTH), F32),
        pltpu.VMEM((tt, HG_WIDTH), F32),
    ]
    return pl.pallas_call(
        functools.partial(_layer_kernel, layer, final_norm),
        grid=(bsz, seq // tt),
        in_specs=in_specs,
        out_specs=pl.BlockSpec((1, tt, D_MODEL), lambda b, t: (b, t, 0)),
        out_shape=jax.ShapeDtypeStruct(x.shape, F32),
        scratch_shapes=scratch,
        compiler_params=pltpu.CompilerParams(
            dimension_semantics=("arbitrary", "arbitrary"),
            vmem_limit_bytes=VMEM_LIMIT_BYTES),
        name=f"trunk_layer{layer}",
    )(*operands)


def kernel(x, p, g_mix, w_in, hg_lb_logits, hg_norm_g, w_br_a, b_glu, conv_w, conv_b, ln_g, ln_b,
           w_br_b, w_out, g_ffn, w_up, ffn_conv_w, ffn_conv_b, w_down, g_ple, w_ple_gate,
           w_ple_proj, g_final):
    row = lambda a: a.reshape(1, -1)
    for i in range(DEPTH):
        small = (hg_lb_logits, row(g_mix[i]), row(hg_norm_g[i]), row(b_glu[i]), conv_w[i],
                 row(conv_b[i]), row(ln_g[i]), row(ln_b[i]), row(g_ffn[i]), ffn_conv_w[i],
                 row(ffn_conv_b[i]), row(g_ple[i]), row(g_final))
        weights = tuple(w[i].astype(BF16) for w in
                        (w_in, w_br_a, w_br_b, w_out, w_up, w_down, w_ple_gate, w_ple_proj))
        x = _layer_call(i, i == DEPTH - 1, x, p, small, weights)
    return x
```

```python
import functools

import jax
import jax.numpy as jnp
from jax import lax
from jax.experimental import pallas as pl
from jax.experimental.pallas import tpu as pltpu

D_MODEL = 1024
DEPTH = 2
CHUNK = 64
PLE_DIM = 256
EPS = 1e-6
HG_HEADS = 4
HG_DK = 128
HG_DV = 128
HG_WIDTH = HG_HEADS * HG_DK
CONV_WIDTH = 512
CONV_K = 31
D_FF = 2816
FFN_CONV_K = 3
IN_COLS = 4 * HG_WIDTH + 2 * CONV_WIDTH + 2 * D_MODEL

OFF_Q = 0
OFF_F = HG_WIDTH
OFF_I = 2 * HG_WIDTH
OFF_OG = 3 * HG_WIDTH
OFF_GLU = 4 * HG_WIDTH
OFF_GA = OFF_GLU + 2 * CONV_WIDTH
OFF_GB = OFF_GA + D_MODEL

SUBLANES = 8
TILE_T = 256
CONV_HALO = 32
FFN_HALO = SUBLANES
FFN_COLS = 256
GATE_COLS = 2 * D_MODEL // SUBLANES
LEVELS = (32, 16, 8, 4, 2, 1)
STAGE_ROWS_WIDE = 32
STAGE_ROWS_SQ = 128
VMEM_LIMIT_BYTES = 58 * 1024 * 1024

F32 = jnp.float32
BF16 = jnp.bfloat16


def _dot(a, b):
    return jnp.dot(a, b, preferred_element_type=F32)


def _dot_nt(a, b):
    return lax.dot_general(a, b, (((1,), (1,)), ((), ())), preferred_element_type=F32)


def _dot_tn(a, b):
    return lax.dot_general(a, b, (((0,), (0,)), ((), ())), preferred_element_type=F32)


def _rms(x, g):
    return x * lax.rsqrt(jnp.mean(x * x, axis=-1, keepdims=True) + EPS) * g


def _sigmoid(x):
    return 0.5 * jnp.tanh(0.5 * x) + 0.5


def _silu(x):
    h = 0.5 * x
    return h * jnp.tanh(h) + h


def _log1p_exp_neg_abs(d):
    return jnp.log(1.0 + jnp.exp(-jnp.abs(d)))


def _log_sigmoid(z):
    return jnp.minimum(z, 0.0) - _log1p_exp_neg_abs(z)


def _split3(g):
    hi = g.astype(BF16)
    r1 = g - hi.astype(F32)
    mid = r1.astype(BF16)
    lo = (r1 - mid.astype(F32)).astype(BF16)
    return hi, mid, lo


def _level_operands(cum, q, kk, half):
    n, w = cum.shape
    if half >= SUBLANES:
        nb = n // (2 * half)
        c = cum.reshape(nb, 2, half, w)
        ref = c[:, 0, half - 1:half, :]
        q_up = q.reshape(nb, 2, half, w)[:, 1] * jnp.exp(c[:, 1] - ref)
        k_lo = kk.reshape(nb, 2, half, w)[:, 0] * jnp.exp(ref - c[:, 0])
        zero = jnp.zeros_like(q_up)
        q_l = jnp.stack([zero, q_up], axis=1).reshape(n, w)
        k_l = jnp.stack([k_lo, zero], axis=1).reshape(n, w)
        return q_l.astype(BF16), k_l.astype(BF16)
    c3 = cum.reshape(n // SUBLANES, SUBLANES, w)
    sub = lax.broadcasted_iota(jnp.int32, c3.shape, 1)
    refs = [jnp.broadcast_to(c3[:, s + half - 1:s + half, :], c3.shape)
            for s in range(0, SUBLANES, 2 * half)]
    ref = refs[-1]
    for i in range(len(refs) - 2, -1, -1):
        ref = jnp.where(sub < (i + 1) * 2 * half, refs[i], ref)
    upper = (sub & half) != 0
    e = jnp.exp(jnp.where(upper, c3 - ref, ref - c3)).reshape(n, w)
    upper = upper.reshape(n, w)
    return (jnp.where(upper, q * e, 0.0).astype(BF16),
            jnp.where(upper, 0.0, kk * e).astype(BF16))


def _hgrn_tile(zq, zf, v, zog, log_lb, log1m_lb, one_m_lb, hg_g, state_ref, oa_ref):
    tt = zq.shape[0]
    nc = tt // CHUNK

    b = log1m_lb + _log_sigmoid(zf)
    log_f = jnp.maximum(log_lb, b) + _log1p_exp_neg_abs(log_lb - b)
    kk = one_m_lb * _sigmoid(-zf)
    q = _silu(zq)

    ti = lax.broadcasted_iota(jnp.int32, (tt, tt), 0)
    si = lax.broadcasted_iota(jnp.int32, (tt, tt), 1)
    tril = ((si <= ti) & ((si ^ ti) < CHUNK)).astype(BF16)
    hi, mid, lo = _split3(log_f)
    cum = _dot(tril, hi) + _dot(tril, mid) + _dot(tril, lo)

    cum4 = cum.reshape(nc, CHUNK // SUBLANES, SUBLANES, HG_WIDTH)
    last4 = cum4[:, CHUNK // SUBLANES - 1:, SUBLANES - 1:, :]
    last = jnp.broadcast_to(last4, cum4.shape).reshape(cum.shape)
    q_dec = (q * jnp.exp(cum)).astype(BF16)
    k_dec = (kk * jnp.exp(last - cum)).astype(BF16)
    chunk_decay = jnp.exp(last4.reshape(nc, HG_WIDTH))
    v_bf = v.astype(BF16)
    qk = q * kk
    gate = _silu(zog)

    ops = [_level_operands(cum, q, kk, half) for half in LEVELS]

    tc = lax.broadcasted_iota(jnp.int32, (CHUNK, CHUNK), 0)
    sc = lax.broadcasted_iota(jnp.int32, (CHUNK, CHUNK), 1)
    txs = tc ^ sc

    kv, scores = {}, {}
    for c in range(nc):
        rows = slice(c * CHUNK, (c + 1) * CHUNK)
        for hh in range(HG_HEADS):
            lanes = slice(hh * HG_DK, (hh + 1) * HG_DK)
            kv[c, hh] = _dot_tn(v_bf[rows, lanes], k_dec[rows, lanes])
            s_acc = None
            for (q_l, k_l), half in zip(ops, LEVELS):
                s_l = jnp.where(txs < 2 * half, _dot_nt(q_l[rows, lanes], k_l[rows, lanes]), 0.0)
                s_acc = s_l if s_acc is None else s_acc + s_l
            scores[c, hh] = s_acc.astype(BF16)

    st_in = {}
    for hh in range(HG_HEADS):
        lanes = slice(hh * HG_DK, (hh + 1) * HG_DK)
        st = state_ref[hh]
        for c in range(nc):
            st_in[c, hh] = st.astype(BF16)
            st = st * chunk_decay[c:c + 1, lanes] + kv[c, hh]
        state_ref[hh] = st

    for c in range(nc):
        rows = slice(c * CHUNK, (c + 1) * CHUNK)
        for hh in range(HG_HEADS):
            lanes = slice(hh * HG_DK, (hh + 1) * HG_DK)
            o = _dot(scores[c, hh], v_bf[rows, lanes])
            o = o + _dot_nt(q_dec[rows, lanes], st_in[c, hh])
            o = o + jnp.sum(qk[rows, lanes], axis=-1, keepdims=True) * v[rows, lanes]
            oa_ref[rows, lanes] = (_rms(o, hg_g) * gate[rows, lanes]).astype(BF16)


def _stage_weight(src, dst, stage, sem):
    rows = stage.shape[1]
    n_chunks = src.shape[0] // rows
    assert n_chunks * rows == src.shape[0] and stage.shape[2] == src.shape[1]

    def chunk_copy(c, slot):
        return pltpu.make_async_copy(src.at[pl.ds(c * rows, rows), :], stage.at[slot],
                                     sem.at[slot])

    chunk_copy(0, 0).start()

    def body(c, carry):
        slot = lax.rem(c, 2)

        @pl.when(c + 1 < n_chunks)
        def _():
            chunk_copy(c + 1, 1 - slot).start()

        chunk_copy(c, slot).wait()
        dst[pl.ds(pl.multiple_of(c * rows, rows), rows), :] = stage[slot].astype(BF16)
        return carry

    lax.fori_loop(0, n_chunks, body, 0)


def _layer_kernel(layer, final_norm,
                  x_ref, p_ref, lbl_ref, g_mix_ref, hg_g_ref, b_glu_ref, cw_ref, cb_ref,
                  ln_g_ref, ln_b_ref, g_ffn_ref, fcw_ref, fcb_ref, g_ple_ref, g_fin_ref,
                  w_in_hbm, w_bra_hbm, w_brb_hbm, w_out_hbm, w_up_hbm, w_down_hbm, w_pg_hbm,
                  w_pp_hbm,
                  o_ref,
                  w_in_ref, w_bra_ref, w_brb_ref, w_out_ref, w_up_ref, w_down_ref, w_pg_ref,
                  w_pp_ref, stage_in, stage_up, stage_sq, stage_sem,
                  state_ref, ubuf_ref, fhalo_ref, act_ref, oa_ref):
    tt = TILE_T

    @pl.when((pl.program_id(0) == 0) & (pl.program_id(1) == 0))
    def _():
        for src, dst, stage in (
                (w_in_hbm, w_in_ref, stage_in), (w_bra_hbm, w_bra_ref, stage_sq),
                (w_brb_hbm, w_brb_ref, stage_sq), (w_out_hbm, w_out_ref, stage_sq),
                (w_up_hbm, w_up_ref, stage_up), (w_down_hbm, w_down_ref, stage_sq),
                (w_pg_hbm, w_pg_ref, stage_sq), (w_pp_hbm, w_pp_ref, stage_sq)):
            _stage_weight(src.at[layer], dst, stage, stage_sem)

    @pl.when(pl.program_id(1) == 0)
    def _():
        state_ref[...] = jnp.zeros_like(state_ref)
        ubuf_ref[0:CONV_HALO, :] = jnp.zeros((CONV_HALO, CONV_WIDTH), F32)
        fhalo_ref[...] = jnp.zeros_like(fhalo_ref)

    x = x_ref[0]
    h = _rms(x, g_mix_ref[...]).astype(BF16)

    lg = lbl_ref[...]
    ex = jnp.exp(lg - jnp.max(lg, axis=0, keepdims=True))
    sm = ex / jnp.sum(ex, axis=0, keepdims=True)
    cs0 = sm[0:1, :]
    cs = cs0
    for j in range(1, layer + 1):
        cs = cs + sm[j:j + 1, :]
    lb = cs - cs0
    log_lb = jnp.log(lb)
    log1m_lb = jnp.log1p(-lb)
    one_m_lb = 1.0 - lb

    zf = _dot(h, w_in_ref[:, OFF_F:OFF_F + HG_WIDTH])
    zq = _dot(h, w_in_ref[:, OFF_Q:OFF_Q + HG_WIDTH])
    zi = _dot(h, w_in_ref[:, OFF_I:OFF_I + HG_WIDTH])
    zog = _dot(h, w_in_ref[:, OFF_OG:OFF_OG + HG_WIDTH])
    _hgrn_tile(zq, zf, zi, zog, log_lb, log1m_lb, one_m_lb, hg_g_ref[...], state_ref, oa_ref)
    y_a = _dot(oa_ref[...], w_bra_ref[...])

    zglu = _dot(h, w_in_ref[:, OFF_GLU:OFF_GA]) + b_glu_ref[...]
    u = zglu[:, :CONV_WIDTH] * _sigmoid(zglu[:, CONV_WIDTH:])
    ubuf_ref[CONV_HALO:CONV_HALO + tt, :] = u
    acc = None
    zg = []
    for r in range(SUBLANES - 1, -1, -1):
        z = None
        for m in range((CONV_K - 1 - r) // SUBLANES + 1):
            j = CONV_K - 1 - r - SUBLANES * m
            start = CONV_HALO - SUBLANES * (m + 1)
            term = cw_ref[j:j + 1, :] * ubuf_ref[start:start + tt + SUBLANES, :]
            z = term if z is None else z + term
        acc = z if acc is None else pltpu.roll(acc, 1, 0) + z
        c0 = OFF_GA + (SUBLANES - 1 - r) * GATE_COLS
        zg.append(_dot(h, w_in_ref[:, c0:c0 + GATE_COLS]))
    zga = jnp.concatenate(zg[:D_MODEL // GATE_COLS], axis=1)
    zgb = jnp.concatenate(zg[D_MODEL // GATE_COLS:], axis=1)
    acc = acc[SUBLANES:, :] + cb_ref[...]
    ubuf_ref[0:CONV_HALO, :] = ubuf_ref[tt:tt + CONV_HALO, :]
    mu = jnp.mean(acc, axis=-1, keepdims=True)
    d = acc - mu
    var = jnp.mean(d * d, axis=-1, keepdims=True)
    ub = _silu(d * lax.rsqrt(var + EPS) * ln_g_ref[...] + ln_b_ref[...])
    y_b = _dot(ub.astype(BF16), w_brb_ref[...])

    y = _sigmoid(zga) * y_a + _sigmoid(zgb) * y_b
    x = x + _dot(y.astype(BF16), w_out_ref[...])

    hf = _rms(x, g_ffn_ref[...]).astype(BF16)

    def ffn_conv(col0):
        cols = slice(col0, col0 + FFN_COLS)
        up = _dot(hf, w_up_ref[:, cols])
        ext = jnp.concatenate([fhalo_ref[:, cols], up], axis=0)
        fhalo_ref[:, cols] = up[tt - FFN_HALO:, :]
        s = fcw_ref[0:1, cols] * ext
        s = pltpu.roll(s, 1, 0) + fcw_ref[1:2, cols] * ext
        s = pltpu.roll(s, 1, 0) + fcw_ref[2:3, cols] * ext
        return s[FFN_HALO:, :] + fcb_ref[:, cols]

    for j in range(D_FF // FFN_COLS):
        ca = ffn_conv(j * FFN_COLS)
        cg = ffn_conv(D_FF + j * FFN_COLS)
        act_ref[:, j * FFN_COLS:(j + 1) * FFN_COLS] = (_silu(ca) * cg).astype(BF16)
    x = x + _dot(act_ref[...], w_down_ref[...])

    hp = _rms(x, g_ple_ref[...]).astype(BF16)
    gate = _sigmoid(_dot(hp, w_pg_ref[...]))
    x = x + gate * _dot(p_ref[0, 0].astype(BF16), w_pp_ref[...])

    if final_norm:
        x = _rms(x, g_fin_ref[...])
    o_ref[0] = x


def _resident(shape):
    nd = len(shape)
    return pl.BlockSpec(shape, lambda b, t: (0,) * nd, pipeline_mode=pl.Buffered(1))


def _layer_call(layer, final_norm, x, p, small, weights):
    bsz, seq, _ = x.shape
    tt = TILE_T
    assert seq % tt == 0 and tt % CHUNK == 0
    operands = (x, p) + tuple(small) + tuple(weights)
    in_specs = (
        [pl.BlockSpec((1, tt, D_MODEL), lambda b, t: (b, t, 0)),
         pl.BlockSpec((1, 1, tt, PLE_DIM), lambda b, t: (layer, b, t, 0))]
        + [_resident(a.shape) for a in small]
        + [pl.BlockSpec(memory_space=pl.ANY) for _ in weights])
    scratch = (
        [pltpu.VMEM(w.shape[1:], BF16) for w in weights]
        + [pltpu.VMEM((2, STAGE_ROWS_WIDE, IN_COLS), F32),
           pltpu.VMEM((2, STAGE_ROWS_WIDE, 2 * D_FF), F32),
           pltpu.VMEM((2, STAGE_ROWS_SQ, D_MODEL), F32),
           pltpu.SemaphoreType.DMA((2,))])
    scratch += [
        pltpu.VMEM((HG_HEADS, HG_DV, HG_DK), F32),
        pltpu.VMEM((CONV_HALO + tt, CONV_WIDTH), F32),
        pltpu.VMEM((FFN_HALO, 2 * D_FF), F32),
        pltpu.VMEM((tt, D_FF), BF16),
        pltpu.VMEM((tt, HG_WIDTH), BF16),
    ]
    return pl.pallas_call(
        functools.partial(_layer_kernel, layer, final_norm),
        grid=(bsz, seq // tt),
        in_specs=in_specs,
        out_specs=pl.BlockSpec((1, tt, D_MODEL), lambda b, t: (b, t, 0)),
        out_shape=jax.ShapeDtypeStruct(x.shape, F32),
        scratch_shapes=scratch,
        compiler_params=pltpu.CompilerParams(
            dimension_semantics=("arbitrary", "arbitrary"),
            vmem_limit_bytes=VMEM_LIMIT_BYTES),
        name=f"trunk_layer{layer}",
    )(*operands)


def kernel(x, p, g_mix, w_in, hg_lb_logits, hg_norm_g, w_br_a, b_glu, conv_w, conv_b, ln_g, ln_b,
           w_br_b, w_out, g_ffn, w_up, ffn_conv_w, ffn_conv_b, w_down, g_ple, w_ple_gate,
           w_ple_proj, g_final):
    row = lambda a: a.reshape(1, -1)
    for i in range(DEPTH):
        small = (hg_lb_logits, row(g_mix[i]), row(hg_norm_g[i]), row(b_glu[i]), conv_w[i],
                 row(conv_b[i]), row(ln_g[i]), row(ln_b[i]), row(g_ffn[i]), ffn_conv_w[i],
                 row(ffn_conv_b[i]), row(g_ple[i]), row(g_final))
        weights = (w_in, w_br_a, w_br_b, w_out, w_up, w_down, w_ple_gate, w_ple_proj)
        x = _layer_call(i, i == DEPTH - 1, x, p, small, weights)
    return x
```

```python
import functools

import jax
import jax.numpy as jnp
from jax import lax
from jax.experimental import pallas as pl
from jax.experimental.pallas import tpu as pltpu

D_MODEL = 1024
DEPTH = 2
CHUNK = 64
PLE_DIM = 256
EPS = 1e-6
HG_HEADS = 4
HG_DK = 128
HG_DV = 128
HG_WIDTH = HG_HEADS * HG_DK
CONV_WIDTH = 512
CONV_K = 31
D_FF = 2816
FFN_CONV_K = 3
IN_COLS = 4 * HG_WIDTH + 2 * CONV_WIDTH + 2 * D_MODEL

OFF_Q = 0
OFF_F = HG_WIDTH
OFF_I = 2 * HG_WIDTH
OFF_OG = 3 * HG_WIDTH
OFF_GLU = 4 * HG_WIDTH
OFF_GA = OFF_GLU + 2 * CONV_WIDTH
OFF_GB = OFF_GA + D_MODEL

SUBLANES = 8
TILE_T = 256
CONV_HALO = 32
FFN_HALO = SUBLANES
FFN_COLS = 256
GATE_COLS = 2 * D_MODEL // SUBLANES
LEVELS = (32, 16, 8, 4, 2, 1)
STAGE_SLOTS = 3
STAGE_ROWS_WIDE = 64
STAGE_ROWS_SQ = 256
VMEM_LIMIT_BYTES = 58 * 1024 * 1024

F32 = jnp.float32
BF16 = jnp.bfloat16


def _dot(a, b):
    return jnp.dot(a, b, preferred_element_type=F32)


def _dot_nt(a, b):
    return lax.dot_general(a, b, (((1,), (1,)), ((), ())), preferred_element_type=F32)


def _dot_tn(a, b):
    return lax.dot_general(a, b, (((0,), (0,)), ((), ())), preferred_element_type=F32)


def _rms(x, g):
    return x * lax.rsqrt(jnp.mean(x * x, axis=-1, keepdims=True) + EPS) * g


def _sigmoid(x):
    return 0.5 * jnp.tanh(0.5 * x) + 0.5


def _silu(x):
    h = 0.5 * x
    return h * jnp.tanh(h) + h


def _log1p_exp_neg_abs(d):
    return jnp.log(1.0 + jnp.exp(-jnp.abs(d)))


def _log_sigmoid(z):
    return jnp.minimum(z, 0.0) - _log1p_exp_neg_abs(z)


def _split3(g):
    hi = g.astype(BF16)
    r1 = g - hi.astype(F32)
    mid = r1.astype(BF16)
    lo = (r1 - mid.astype(F32)).astype(BF16)
    return hi, mid, lo


def _level_operands(cum, q, kk, half):
    n, w = cum.shape
    if half >= SUBLANES:
        nb = n // (2 * half)
        c = cum.reshape(nb, 2, half, w)
        ref = c[:, 0, half - 1:half, :]
        q_up = q.reshape(nb, 2, half, w)[:, 1] * jnp.exp(c[:, 1] - ref)
        k_lo = kk.reshape(nb, 2, half, w)[:, 0] * jnp.exp(ref - c[:, 0])
        zero = jnp.zeros_like(q_up)
        q_l = jnp.stack([zero, q_up], axis=1).reshape(n, w)
        k_l = jnp.stack([k_lo, zero], axis=1).reshape(n, w)
        return q_l.astype(BF16), k_l.astype(BF16)
    c3 = cum.reshape(n // SUBLANES, SUBLANES, w)
    sub = lax.broadcasted_iota(jnp.int32, c3.shape, 1)
    refs = [jnp.broadcast_to(c3[:, s + half - 1:s + half, :], c3.shape)
            for s in range(0, SUBLANES, 2 * half)]
    ref = refs[-1]
    for i in range(len(refs) - 2, -1, -1):
        ref = jnp.where(sub < (i + 1) * 2 * half, refs[i], ref)
    upper = (sub & half) != 0
    e = jnp.exp(jnp.where(upper, c3 - ref, ref - c3)).reshape(n, w)
    upper = upper.reshape(n, w)
    return (jnp.where(upper, q * e, 0.0).astype(BF16),
            jnp.where(upper, 0.0, kk * e).astype(BF16))


def _hgrn_tile(zq, zf, v, zog, log_lb, log1m_lb, one_m_lb, hg_g, state_ref, oa_ref):
    tt = zq.shape[0]
    nc = tt // CHUNK

    b = log1m_lb + _log_sigmoid(zf)
    log_f = jnp.maximum(log_lb, b) + _log1p_exp_neg_abs(log_lb - b)
    kk = one_m_lb * _sigmoid(-zf)
    q = _silu(zq)

    ti = lax.broadcasted_iota(jnp.int32, (tt, tt), 0)
    si = lax.broadcasted_iota(jnp.int32, (tt, tt), 1)
    tril = ((si <= ti) & ((si ^ ti) < CHUNK)).astype(BF16)
    hi, mid, lo = _split3(log_f)
    cum = _dot(tril, hi) + _dot(tril, mid) + _dot(tril, lo)

    cum4 = cum.reshape(nc, CHUNK // SUBLANES, SUBLANES, HG_WIDTH)
    last4 = cum4[:, CHUNK // SUBLANES - 1:, SUBLANES - 1:, :]
    last = jnp.broadcast_to(last4, cum4.shape).reshape(cum.shape)
    q_dec = (q * jnp.exp(cum)).astype(BF16)
    k_dec = (kk * jnp.exp(last - cum)).astype(BF16)
    chunk_decay = jnp.exp(last4.reshape(nc, HG_WIDTH))
    v_bf = v.astype(BF16)
    qk = q * kk
    gate = _silu(zog)

    ops = [_level_operands(cum, q, kk, half) for half in LEVELS]

    tc = lax.broadcasted_iota(jnp.int32, (CHUNK, CHUNK), 0)
    sc = lax.broadcasted_iota(jnp.int32, (CHUNK, CHUNK), 1)
    txs = tc ^ sc

    kv, scores = {}, {}
    for c in range(nc):
        rows = slice(c * CHUNK, (c + 1) * CHUNK)
        for hh in range(HG_HEADS):
            lanes = slice(hh * HG_DK, (hh + 1) * HG_DK)
            kv[c, hh] = _dot_tn(v_bf[rows, lanes], k_dec[rows, lanes])
            s_acc = None
            for (q_l, k_l), half in zip(ops, LEVELS):
                s_l = _dot_nt(q_l[rows, lanes], k_l[rows, lanes])
                if 2 * half < CHUNK:
                    s_l = jnp.where(txs < 2 * half, s_l, 0.0)
                s_acc = s_l if s_acc is None else s_acc + s_l
            scores[c, hh] = s_acc.astype(BF16)

    st_in = {}
    for hh in range(HG_HEADS):
        lanes = slice(hh * HG_DK, (hh + 1) * HG_DK)
        st = state_ref[hh]
        for c in range(nc):
            st_in[c, hh] = st.astype(BF16)
            st = st * chunk_decay[c:c + 1, lanes] + kv[c, hh]
        state_ref[hh] = st

    for c in range(nc):
        rows = slice(c * CHUNK, (c + 1) * CHUNK)
        for hh in range(HG_HEADS):
            lanes = slice(hh * HG_DK, (hh + 1) * HG_DK)
            o = _dot(scores[c, hh], v_bf[rows, lanes])
            o = o + _dot_nt(q_dec[rows, lanes], st_in[c, hh])
            o = o + jnp.sum(qk[rows, lanes], axis=-1, keepdims=True) * v[rows, lanes]
            oa_ref[rows, lanes] = (_rms(o, hg_g) * gate[rows, lanes]).astype(BF16)


def _stage_weights(layer, plan):
    per_stage = {}
    for src, dst, stage, sem in plan:
        slots, rows, width = stage.shape
        k, n = src.shape[1:]
        assert k % rows == 0 and n <= width
        ring = per_stage.setdefault(id(stage), [])
        for row0 in range(0, k, rows):
            ring.append((src, dst, stage, sem, len(ring) % slots, row0, rows, n))
    jobs = []
    rings = list(per_stage.values())
    for i in range(max(len(r) for r in rings)):
        jobs.extend(r[i] for r in rings if i < len(r))

    def job_copy(j):
        src, _, stage, sem, slot, row0, rows, n = jobs[j]
        return pltpu.make_async_copy(src.at[layer, pl.ds(row0, rows), :],
                                     stage.at[slot, :, pl.ds(0, n)], sem.at[slot])

    last_user, free_after = {}, []
    for j, (_, _, stage, _, slot, _, _, _) in enumerate(jobs):
        free_after.append(last_user.get((id(stage), slot), -1))
        last_user[id(stage), slot] = j

    started = 0
    for j in range(len(jobs)):
        while started < len(jobs) and free_after[started] < j:
            job_copy(started).start()
            started += 1
        job_copy(j).wait()
        _, dst, stage, _, slot, row0, rows, n = jobs[j]
        dst[row0:row0 + rows, :] = stage[slot, :, 0:n].astype(BF16)


def _layer_kernel(layer, final_norm,
                  x_ref, p_ref, lbl_ref, g_mix_ref, hg_g_ref, b_glu_ref, cw_ref, cb_ref,
                  ln_g_ref, ln_b_ref, g_ffn_ref, fcw_ref, fcb_ref, g_ple_ref, g_fin_ref,
                  w_in_hbm, w_bra_hbm, w_brb_hbm, w_out_hbm, w_up_hbm, w_down_hbm, w_pg_hbm,
                  w_pp_hbm,
                  o_ref,
                  w_in_ref, w_bra_ref, w_brb_ref, w_out_ref, w_up_ref, w_down_ref, w_pg_ref,
                  w_pp_ref, stage_wide, stage_sq, sem_wide, sem_sq,
                  state_ref, ubuf_ref, fhalo_ref, act_ref, oa_ref):
    tt = TILE_T

    @pl.when((pl.program_id(0) == 0) & (pl.program_id(1) == 0))
    def _():
        wide = (stage_wide, sem_wide)
        sq = (stage_sq, sem_sq)
        _stage_weights(layer, [
            (w_in_hbm, w_in_ref) + wide, (w_up_hbm, w_up_ref) + wide,
            (w_bra_hbm, w_bra_ref) + sq, (w_brb_hbm, w_brb_ref) + sq,
            (w_out_hbm, w_out_ref) + sq, (w_down_hbm, w_down_ref) + sq,
            (w_pg_hbm, w_pg_ref) + sq, (w_pp_hbm, w_pp_ref) + sq])

    @pl.when(pl.program_id(1) == 0)
    def _():
        state_ref[...] = jnp.zeros_like(state_ref)
        ubuf_ref[0:CONV_HALO, :] = jnp.zeros((CONV_HALO, CONV_WIDTH), F32)
        fhalo_ref[...] = jnp.zeros_like(fhalo_ref)

    x = x_ref[0]
    h = _rms(x, g_mix_ref[...]).astype(BF16)

    lg = lbl_ref[...]
    ex = jnp.exp(lg - jnp.max(lg, axis=0, keepdims=True))
    sm = ex / jnp.sum(ex, axis=0, keepdims=True)
    cs0 = sm[0:1, :]
    cs = cs0
    for j in range(1, layer + 1):
        cs = cs + sm[j:j + 1, :]
    lb = cs - cs0
    log_lb = jnp.log(lb)
    log1m_lb = jnp.log1p(-lb)
    one_m_lb = 1.0 - lb

    zf = _dot(h, w_in_ref[:, OFF_F:OFF_F + HG_WIDTH])
    zq = _dot(h, w_in_ref[:, OFF_Q:OFF_Q + HG_WIDTH])
    zi = _dot(h, w_in_ref[:, OFF_I:OFF_I + HG_WIDTH])
    zog = _dot(h, w_in_ref[:, OFF_OG:OFF_OG + HG_WIDTH])
    _hgrn_tile(zq, zf, zi, zog, log_lb, log1m_lb, one_m_lb, hg_g_ref[...], state_ref, oa_ref)
    y_a = _dot(oa_ref[...], w_bra_ref[...])

    zglu = _dot(h, w_in_ref[:, OFF_GLU:OFF_GA]) + b_glu_ref[...]
    u = zglu[:, :CONV_WIDTH] * _sigmoid(zglu[:, CONV_WIDTH:])
    ubuf_ref[CONV_HALO:CONV_HALO + tt, :] = u
    acc = None
    zg = []
    for r in range(SUBLANES - 1, -1, -1):
        z = None
        for m in range((CONV_K - 1 - r) // SUBLANES + 1):
            j = CONV_K - 1 - r - SUBLANES * m
            start = CONV_HALO - SUBLANES * (m + 1)
            term = cw_ref[j:j + 1, :] * ubuf_ref[start:start + tt + SUBLANES, :]
            z = term if z is None else z + term
        acc = z if acc is None else pltpu.roll(acc, 1, 0) + z
        c0 = OFF_GA + (SUBLANES - 1 - r) * GATE_COLS
        zg.append(_dot(h, w_in_ref[:, c0:c0 + GATE_COLS]))
    zga = jnp.concatenate(zg[:D_MODEL // GATE_COLS], axis=1)
    zgb = jnp.concatenate(zg[D_MODEL // GATE_COLS:], axis=1)
    acc = acc[SUBLANES:, :] + cb_ref[...]
    ubuf_ref[0:CONV_HALO, :] = ubuf_ref[tt:tt + CONV_HALO, :]
    mu = jnp.mean(acc, axis=-1, keepdims=True)
    d = acc - mu
    var = jnp.mean(d * d, axis=-1, keepdims=True)
    ub = _silu(d * lax.rsqrt(var + EPS) * ln_g_ref[...] + ln_b_ref[...])
    y_b = _dot(ub.astype(BF16), w_brb_ref[...])

    y = _sigmoid(zga) * y_a + _sigmoid(zgb) * y_b
    x = x + _dot(y.astype(BF16), w_out_ref[...])

    hf = _rms(x, g_ffn_ref[...]).astype(BF16)

    def ffn_conv(col0):
        cols = slice(col0, col0 + FFN_COLS)
        up = _dot(hf, w_up_ref[:, cols])
        ext = jnp.concatenate([fhalo_ref[:, cols], up], axis=0)
        fhalo_ref[:, cols] = up[tt - FFN_HALO:, :]
        s = fcw_ref[0:1, cols] * ext
        s = pltpu.roll(s, 1, 0) + fcw_ref[1:2, cols] * ext
        s = pltpu.roll(s, 1, 0) + fcw_ref[2:3, cols] * ext
        return s[FFN_HALO:, :] + fcb_ref[:, cols]

    for j in range(D_FF // FFN_COLS):
        ca = ffn_conv(j * FFN_COLS)
        cg = ffn_conv(D_FF + j * FFN_COLS)
        act_ref[:, j * FFN_COLS:(j + 1) * FFN_COLS] = (_silu(ca) * cg).astype(BF16)
    x = x + _dot(act_ref[...], w_down_ref[...])

    hp = _rms(x, g_ple_ref[...]).astype(BF16)
    gate = _sigmoid(_dot(hp, w_pg_ref[...]))
    x = x + gate * _dot(p_ref[0, 0].astype(BF16), w_pp_ref[...])

    if final_norm:
        x = _rms(x, g_fin_ref[...])
    o_ref[0] = x


def _resident(shape):
    nd = len(shape)
    return pl.BlockSpec(shape, lambda b, t: (0,) * nd, pipeline_mode=pl.Buffered(1))


def _layer_call(layer, final_norm, x, p, small, weights):
    bsz, seq, _ = x.shape
    tt = TILE_T
    assert seq % tt == 0 and tt % CHUNK == 0
    operands = (x, p) + tuple(small) + tuple(weights)
    in_specs = (
        [pl.BlockSpec((1, tt, D_MODEL), lambda b, t: (b, t, 0)),
         pl.BlockSpec((1, 1, tt, PLE_DIM), lambda b, t: (layer, b, t, 0))]
        + [_resident(a.shape) for a in small]
        + [pl.BlockSpec(memory_space=pl.ANY) for _ in weights])
    scratch = (
        [pltpu.VMEM(w.shape[1:], BF16) for w in weights]
        + [pltpu.VMEM((STAGE_SLOTS, STAGE_ROWS_WIDE, max(IN_COLS, 2 * D_FF)), F32),
           pltpu.VMEM((STAGE_SLOTS, STAGE_ROWS_SQ, D_MODEL), F32),
           pltpu.SemaphoreType.DMA((STAGE_SLOTS,)),
           pltpu.SemaphoreType.DMA((STAGE_SLOTS,))])
    scratch += [
        pltpu.VMEM((HG_HEADS, HG_DV, HG_DK), F32),
        pltpu.VMEM((CONV_HALO + tt, CONV_WIDTH), F32),
        pltpu.VMEM((FFN_HALO, 2 * D_FF), F32),
        pltpu.VMEM((tt, D_FF), BF16),
        pltpu.VMEM((tt, HG_WIDTH), BF16),
    ]
    return pl.pallas_call(
        functools.partial(_layer_kernel, layer, final_norm),
        grid=(bsz, seq // tt),
        in_specs=in_specs,
        out_specs=pl.BlockSpec((1, tt, D_MODEL), lambda b, t: (b, t, 0)),
        out_shape=jax.ShapeDtypeStruct(x.shape, F32),
        scratch_shapes=scratch,
        compiler_params=pltpu.CompilerParams(
            dimension_semantics=("arbitrary", "arbitrary"),
            vmem_limit_bytes=VMEM_LIMIT_BYTES),
        name=f"trunk_layer{layer}",
    )(*operands)


def kernel(x, p, g_mix, w_in, hg_lb_logits, hg_norm_g, w_br_a, b_glu, conv_w, conv_b, ln_g, ln_b,
           w_br_b, w_out, g_ffn, w_up, ffn_conv_w, ffn_conv_b, w_down, g_ple, w_ple_gate,
           w_ple_proj, g_final):
    row = lambda a: a.reshape(1, -1)
    for i in range(DEPTH):
        small = (hg_lb_logits, row(g_mix[i]), row(hg_norm_g[i]), row(b_glu[i]), conv_w[i],
                 row(conv_b[i]), row(ln_g[i]), row(ln_b[i]), row(g_ffn[i]), ffn_conv_w[i],
                 row(ffn_conv_b[i]), row(g_ple[i]), row(g_final))
        weights = (w_in, w_br_a, w_br_b, w_out, w_up, w_down, w_ple_gate, w_ple_proj)
        x = _layer_call(i, i == DEPTH - 1, x, p, small, weights)
    return x
```

```python
import functools

import jax
import jax.numpy as jnp
from jax import lax
from jax.experimental import pallas as pl
from jax.experimental.pallas import tpu as pltpu

D_MODEL = 1024
DEPTH = 2
CHUNK = 64
PLE_DIM = 256
EPS = 1e-6
HG_HEADS = 4
HG_DK = 128
HG_DV = 128
HG_WIDTH = HG_HEADS * HG_DK
CONV_WIDTH = 512
CONV_K = 31
D_FF = 2816
FFN_CONV_K = 3
IN_COLS = 4 * HG_WIDTH + 2 * CONV_WIDTH + 2 * D_MODEL

OFF_Q = 0
OFF_F = HG_WIDTH
OFF_I = 2 * HG_WIDTH
OFF_OG = 3 * HG_WIDTH
OFF_GLU = 4 * HG_WIDTH
OFF_GA = OFF_GLU + 2 * CONV_WIDTH
OFF_GB = OFF_GA + D_MODEL

SUBLANES = 8
TILE_T = 256
CONV_HALO = 32
FFN_HALO = SUBLANES
FFN_COLS = 256
LANES = 128
CONV_ROWS = 128
GATE_BLOCKS = (CONV_WIDTH // LANES) * (TILE_T // CONV_ROWS)
GATE_COLS = 2 * D_MODEL // GATE_BLOCKS
LEVELS = (32, 16, 8, 4, 2, 1)
STAGE_SLOTS = 3
STAGE_ROWS_WIDE = 64
STAGE_ROWS_SQ = 256
VMEM_LIMIT_BYTES = 58 * 1024 * 1024

F32 = jnp.float32
BF16 = jnp.bfloat16


def _dot(a, b):
    return jnp.dot(a, b, preferred_element_type=F32)


def _dot_nt(a, b):
    return lax.dot_general(a, b, (((1,), (1,)), ((), ())), preferred_element_type=F32)


def _dot_tn(a, b):
    return lax.dot_general(a, b, (((0,), (0,)), ((), ())), preferred_element_type=F32)


def _rms(x, g):
    return x * lax.rsqrt(jnp.mean(x * x, axis=-1, keepdims=True) + EPS) * g


def _sigmoid(x):
    return 0.5 * jnp.tanh(0.5 * x) + 0.5


def _silu(x):
    h = 0.5 * x
    return h * jnp.tanh(h) + h


def _log1p_exp_neg_abs(d):
    return jnp.log(1.0 + jnp.exp(-jnp.abs(d)))


def _log_sigmoid(z):
    return jnp.minimum(z, 0.0) - _log1p_exp_neg_abs(z)


def _split3(g):
    hi = g.astype(BF16)
    r1 = g - hi.astype(F32)
    mid = r1.astype(BF16)
    lo = (r1 - mid.astype(F32)).astype(BF16)
    return hi, mid, lo


def _level_operands(cum, q, kk, half):
    n, w = cum.shape
    if half >= SUBLANES:
        nb = n // (2 * half)
        c = cum.reshape(nb, 2, half, w)
        ref = c[:, 0, half - 1:half, :]
        q_up = q.reshape(nb, 2, half, w)[:, 1] * jnp.exp(c[:, 1] - ref)
        k_lo = kk.reshape(nb, 2, half, w)[:, 0] * jnp.exp(ref - c[:, 0])
        zero = jnp.zeros_like(q_up)
        q_l = jnp.stack([zero, q_up], axis=1).reshape(n, w)
        k_l = jnp.stack([k_lo, zero], axis=1).reshape(n, w)
        return q_l.astype(BF16), k_l.astype(BF16)
    c3 = cum.reshape(n // SUBLANES, SUBLANES, w)
    sub = lax.broadcasted_iota(jnp.int32, c3.shape, 1)
    refs = [jnp.broadcast_to(c3[:, s + half - 1:s + half, :], c3.shape)
            for s in range(0, SUBLANES, 2 * half)]
    ref = refs[-1]
    for i in range(len(refs) - 2, -1, -1):
        ref = jnp.where(sub < (i + 1) * 2 * half, refs[i], ref)
    upper = (sub & half) != 0
    e = jnp.exp(jnp.where(upper, c3 - ref, ref - c3)).reshape(n, w)
    upper = upper.reshape(n, w)
    return (jnp.where(upper, q * e, 0.0).astype(BF16),
            jnp.where(upper, 0.0, kk * e).astype(BF16))


def _hgrn_tile(zq, zf, v, zog, log_lb, log1m_lb, one_m_lb, hg_g, state_ref, oa_ref):
    tt = zq.shape[0]
    nc = tt // CHUNK

    b = log1m_lb + _log_sigmoid(zf)
    log_f = jnp.maximum(log_lb, b) + _log1p_exp_neg_abs(log_lb - b)
    kk = one_m_lb * _sigmoid(-zf)
    q = _silu(zq)

    ti = lax.broadcasted_iota(jnp.int32, (tt, tt), 0)
    si = lax.broadcasted_iota(jnp.int32, (tt, tt), 1)
    tril = ((si <= ti) & ((si ^ ti) < CHUNK)).astype(BF16)
    hi, mid, lo = _split3(log_f)
    cum = _dot(tril, hi) + _dot(tril, mid) + _dot(tril, lo)

    cum4 = cum.reshape(nc, CHUNK // SUBLANES, SUBLANES, HG_WIDTH)
    last4 = cum4[:, CHUNK // SUBLANES - 1:, SUBLANES - 1:, :]
    last = jnp.broadcast_to(last4, cum4.shape).reshape(cum.shape)
    q_dec = (q * jnp.exp(cum)).astype(BF16)
    k_dec = (kk * jnp.exp(last - cum)).astype(BF16)
    chunk_decay = jnp.exp(last4.reshape(nc, HG_WIDTH))
    v_bf = v.astype(BF16)
    qk = q * kk
    gate = _silu(zog)

    ops = [_level_operands(cum, q, kk, half) for half in LEVELS]

    tc = lax.broadcasted_iota(jnp.int32, (CHUNK, CHUNK), 0)
    sc = lax.broadcasted_iota(jnp.int32, (CHUNK, CHUNK), 1)
    txs = tc ^ sc

    kv, scores = {}, {}
    for c in range(nc):
        rows = slice(c * CHUNK, (c + 1) * CHUNK)
        for hh in range(HG_HEADS):
            lanes = slice(hh * HG_DK, (hh + 1) * HG_DK)
            kv[c, hh] = _dot_tn(v_bf[rows, lanes], k_dec[rows, lanes])
            s_acc = None
            for (q_l, k_l), half in zip(ops, LEVELS):
                s_l = _dot_nt(q_l[rows, lanes], k_l[rows, lanes])
                if 2 * half < CHUNK:
                    s_l = jnp.where(txs < 2 * half, s_l, 0.0)
                s_acc = s_l if s_acc is None else s_acc + s_l
            scores[c, hh] = s_acc.astype(BF16)

    st_in = {}
    for hh in range(HG_HEADS):
        lanes = slice(hh * HG_DK, (hh + 1) * HG_DK)
        st = state_ref[hh]
        for c in range(nc):
            st_in[c, hh] = st.astype(BF16)
            st = st * chunk_decay[c:c + 1, lanes] + kv[c, hh]
        state_ref[hh] = st

    for c in range(nc):
        rows = slice(c * CHUNK, (c + 1) * CHUNK)
        for hh in range(HG_HEADS):
            lanes = slice(hh * HG_DK, (hh + 1) * HG_DK)
            o = _dot(scores[c, hh], v_bf[rows, lanes])
            o = o + _dot_nt(q_dec[rows, lanes], st_in[c, hh])
            o = o + jnp.sum(qk[rows, lanes], axis=-1, keepdims=True) * v[rows, lanes]
            oa_ref[rows, lanes] = (_rms(o, hg_g) * gate[rows, lanes]).astype(BF16)


def _stage_weights(layer, plan):
    per_stage = {}
    for src, dst, stage, sem in plan:
        slots, rows, width = stage.shape
        k, n = src.shape[1:]
        assert k % rows == 0 and n <= width
        ring = per_stage.setdefault(id(stage), [])
        for row0 in range(0, k, rows):
            ring.append((src, dst, stage, sem, len(ring) % slots, row0, rows, n))
    jobs = []
    rings = list(per_stage.values())
    for i in range(max(len(r) for r in rings)):
        jobs.extend(r[i] for r in rings if i < len(r))

    def job_copy(j):
        src, _, stage, sem, slot, row0, rows, n = jobs[j]
        return pltpu.make_async_copy(src.at[layer, pl.ds(row0, rows), :],
                                     stage.at[slot, :, pl.ds(0, n)], sem.at[slot])

    last_user, free_after = {}, []
    for j, (_, _, stage, _, slot, _, _, _) in enumerate(jobs):
        free_after.append(last_user.get((id(stage), slot), -1))
        last_user[id(stage), slot] = j

    started = 0
    for j in range(len(jobs)):
        while started < len(jobs) and free_after[started] < j:
            job_copy(started).start()
            started += 1
        job_copy(j).wait()
        _, dsts, stage, _, slot, row0, rows, n = jobs[j]
        assert sum(ncols for _, _, _, ncols in dsts) == n
        for ref, lead, col0, ncols in dsts:
            chunk = stage[slot, :, col0:col0 + ncols].astype(BF16)
            if lead is None:
                ref[row0:row0 + rows, :] = chunk
            else:
                ref[lead, row0:row0 + rows, :] = chunk


def _layer_kernel(layer, final_norm,
                  x_ref, p_ref, lbl_ref, g_mix_ref, hg_g_ref, b_glu_ref, cw_ref, cb_ref,
                  ln_g_ref, ln_b_ref, g_ffn_ref, fcw_ref, fcb_ref, g_ple_ref, g_fin_ref,
                  w_in_hbm, w_bra_hbm, w_brb_hbm, w_out_hbm, w_up_hbm, w_down_hbm, w_pg_hbm,
                  w_pp_hbm,
                  o_ref,
                  w_in_ref, w_bra_ref, w_brb_ref, w_out_ref, w_up_ref, w_down_ref, w_pg_ref,
                  w_pp_ref, w_gate_ref, stage_wide, stage_sq, sem_wide, sem_sq,
                  state_ref, ubuf_ref, fhalo_ref, act_ref, oa_ref, cv_ref, zg_ref):
    tt = TILE_T
    lane_tiles = CONV_WIDTH // LANES

    @pl.when((pl.program_id(0) == 0) & (pl.program_id(1) == 0))
    def _():
        wide = (stage_wide, sem_wide)
        sq = (stage_sq, sem_sq)
        whole = lambda ref: [(ref, None, 0, ref.shape[1])]
        w_in_dsts = whole(w_in_ref) + [(w_gate_ref, g, OFF_GA + g * GATE_COLS, GATE_COLS)
                                       for g in range(GATE_BLOCKS)]
        _stage_weights(layer, [
            (w_in_hbm, w_in_dsts) + wide, (w_up_hbm, whole(w_up_ref)) + wide,
            (w_bra_hbm, whole(w_bra_ref)) + sq, (w_brb_hbm, whole(w_brb_ref)) + sq,
            (w_out_hbm, whole(w_out_ref)) + sq, (w_down_hbm, whole(w_down_ref)) + sq,
            (w_pg_hbm, whole(w_pg_ref)) + sq, (w_pp_hbm, whole(w_pp_ref)) + sq])

    @pl.when(pl.program_id(1) == 0)
    def _():
        state_ref[...] = jnp.zeros_like(state_ref)
        ubuf_ref[:, 0:CONV_HALO, :] = jnp.zeros((lane_tiles, CONV_HALO, LANES), F32)
        fhalo_ref[...] = jnp.zeros_like(fhalo_ref)

    x = x_ref[0]
    h = _rms(x, g_mix_ref[...]).astype(BF16)

    lg = lbl_ref[...]
    ex = jnp.exp(lg - jnp.max(lg, axis=0, keepdims=True))
    sm = ex / jnp.sum(ex, axis=0, keepdims=True)
    cs0 = sm[0:1, :]
    cs = cs0
    for j in range(1, layer + 1):
        cs = cs + sm[j:j + 1, :]
    lb = cs - cs0
    log_lb = jnp.log(lb)
    log1m_lb = jnp.log1p(-lb)
    one_m_lb = 1.0 - lb

    zf = _dot(h, w_in_ref[:, OFF_F:OFF_F + HG_WIDTH])
    zq = _dot(h, w_in_ref[:, OFF_Q:OFF_Q + HG_WIDTH])
    zi = _dot(h, w_in_ref[:, OFF_I:OFF_I + HG_WIDTH])
    zog = _dot(h, w_in_ref[:, OFF_OG:OFF_OG + HG_WIDTH])
    _hgrn_tile(zq, zf, zi, zog, log_lb, log1m_lb, one_m_lb, hg_g_ref[...], state_ref, oa_ref)
    y_a = _dot(oa_ref[...], w_bra_ref[...])

    zglu = _dot(h, w_in_ref[:, OFF_GLU:OFF_GA]) + b_glu_ref[...]
    u = zglu[:, :CONV_WIDTH] * _sigmoid(zglu[:, CONV_WIDTH:])
    for lt in range(lane_tiles):
        ubuf_ref[lt, CONV_HALO:CONV_HALO + tt, :] = u[:, lt * LANES:(lt + 1) * LANES]

    def conv_lane_tile(lt, carry):
        for blk in range(tt // CONV_ROWS):
            r0 = blk * CONV_ROWS
            acc = None
            for r in range(SUBLANES - 1, -1, -1):
                z = None
                for m in range((CONV_K - 1 - r) // SUBLANES + 1):
                    j = CONV_K - 1 - r - SUBLANES * m
                    start = CONV_HALO - SUBLANES * (m + 1) + r0
                    term = (cw_ref[lt, j:j + 1, :]
                            * ubuf_ref[lt, start:start + CONV_ROWS + SUBLANES, :])
                    z = term if z is None else z + term
                acc = z if acc is None else pltpu.roll(acc, 1, 0) + z
            cv_ref[lt, r0:r0 + CONV_ROWS, :] = acc[SUBLANES:, :] + cb_ref[lt]
            g = lt * (tt // CONV_ROWS) + blk
            zg_ref[g] = _dot(h, w_gate_ref[g])
        return carry

    lax.fori_loop(0, lane_tiles, conv_lane_tile, 0)
    zga = jnp.concatenate([zg_ref[g] for g in range(GATE_BLOCKS // 2)], axis=1)
    zgb = jnp.concatenate([zg_ref[g] for g in range(GATE_BLOCKS // 2, GATE_BLOCKS)], axis=1)
    acc = jnp.concatenate([cv_ref[lt] for lt in range(lane_tiles)], axis=1)
    for lt in range(lane_tiles):
        ubuf_ref[lt, 0:CONV_HALO, :] = ubuf_ref[lt, tt:tt + CONV_HALO, :]
    mu = jnp.mean(acc, axis=-1, keepdims=True)
    d = acc - mu
    var = jnp.mean(d * d, axis=-1, keepdims=True)
    ub = _silu(d * lax.rsqrt(var + EPS) * ln_g_ref[...] + ln_b_ref[...])
    y_b = _dot(ub.astype(BF16), w_brb_ref[...])

    y = _sigmoid(zga) * y_a + _sigmoid(zgb) * y_b
    x = x + _dot(y.astype(BF16), w_out_ref[...])

    hf = _rms(x, g_ffn_ref[...]).astype(BF16)

    def ffn_conv(col0):
        cols = slice(col0, col0 + FFN_COLS)
        up = _dot(hf, w_up_ref[:, cols])
        ext = jnp.concatenate([fhalo_ref[:, cols], up], axis=0)
        fhalo_ref[:, cols] = up[tt - FFN_HALO:, :]
        s = fcw_ref[0:1, cols] * ext
        s = pltpu.roll(s, 1, 0) + fcw_ref[1:2, cols] * ext
        s = pltpu.roll(s, 1, 0) + fcw_ref[2:3, cols] * ext
        return s[FFN_HALO:, :] + fcb_ref[:, cols]

    for j in range(D_FF // FFN_COLS):
        ca = ffn_conv(j * FFN_COLS)
        cg = ffn_conv(D_FF + j * FFN_COLS)
        act_ref[:, j * FFN_COLS:(j + 1) * FFN_COLS] = (_silu(ca) * cg).astype(BF16)
    x = x + _dot(act_ref[...], w_down_ref[...])

    hp = _rms(x, g_ple_ref[...]).astype(BF16)
    gate = _sigmoid(_dot(hp, w_pg_ref[...]))
    x = x + gate * _dot(p_ref[0, 0].astype(BF16), w_pp_ref[...])

    if final_norm:
        x = _rms(x, g_fin_ref[...])
    o_ref[0] = x


def _resident(shape):
    nd = len(shape)
    return pl.BlockSpec(shape, lambda b, t: (0,) * nd, pipeline_mode=pl.Buffered(1))


def _layer_call(layer, final_norm, x, p, small, weights):
    bsz, seq, _ = x.shape
    tt = TILE_T
    assert seq % tt == 0 and tt % CHUNK == 0
    operands = (x, p) + tuple(small) + tuple(weights)
    in_specs = (
        [pl.BlockSpec((1, tt, D_MODEL), lambda b, t: (b, t, 0)),
         pl.BlockSpec((1, 1, tt, PLE_DIM), lambda b, t: (layer, b, t, 0))]
        + [_resident(a.shape) for a in small]
        + [pl.BlockSpec(memory_space=pl.ANY) for _ in weights])
    w_in = weights[0]
    resident = [(w_in.shape[1], OFF_GA)] + [w.shape[1:] for w in weights[1:]]
    scratch = (
        [pltpu.VMEM(shape, BF16) for shape in resident]
        + [pltpu.VMEM((GATE_BLOCKS, D_MODEL, GATE_COLS), BF16),
           pltpu.VMEM((STAGE_SLOTS, STAGE_ROWS_WIDE, max(IN_COLS, 2 * D_FF)), F32),
           pltpu.VMEM((STAGE_SLOTS, STAGE_ROWS_SQ, D_MODEL), F32),
           pltpu.SemaphoreType.DMA((STAGE_SLOTS,)),
           pltpu.SemaphoreType.DMA((STAGE_SLOTS,))])
    scratch += [
        pltpu.VMEM((HG_HEADS, HG_DV, HG_DK), F32),
        pltpu.VMEM((CONV_WIDTH // LANES, CONV_HALO + tt, LANES), F32),
        pltpu.VMEM((FFN_HALO, 2 * D_FF), F32),
        pltpu.VMEM((tt, D_FF), BF16),
        pltpu.VMEM((tt, HG_WIDTH), BF16),
        pltpu.VMEM((CONV_WIDTH // LANES, tt, LANES), F32),
        pltpu.VMEM((GATE_BLOCKS, tt, GATE_COLS), F32),
    ]
    return pl.pallas_call(
        functools.partial(_layer_kernel, layer, final_norm),
        grid=(bsz, seq // tt),
        in_specs=in_specs,
        out_specs=pl.BlockSpec((1, tt, D_MODEL), lambda b, t: (b, t, 0)),
        out_shape=jax.ShapeDtypeStruct(x.shape, F32),
        scratch_shapes=scratch,
        compiler_params=pltpu.CompilerParams(
            dimension_semantics=("arbitrary", "arbitrary"),
            vmem_limit_bytes=VMEM_LIMIT_BYTES),
        name=f"trunk_layer{layer}",
    )(*operands)


def kernel(x, p, g_mix, w_in, hg_lb_logits, hg_norm_g, w_br_a, b_glu, conv_w, conv_b, ln_g, ln_b,
           w_br_b, w_out, g_ffn, w_up, ffn_conv_w, ffn_conv_b, w_down, g_ple, w_ple_gate,
           w_ple_proj, g_final):
    row = lambda a: a.reshape(1, -1)
    lane_tiled = lambda a: a.reshape(a.shape[0], CONV_WIDTH // LANES, LANES).transpose(1, 0, 2)
    for i in range(DEPTH):
        small = (hg_lb_logits, row(g_mix[i]), row(hg_norm_g[i]), row(b_glu[i]),
                 lane_tiled(conv_w[i]),
                 lane_tiled(row(conv_b[i])), row(ln_g[i]), row(ln_b[i]), row(g_ffn[i]), ffn_conv_w[i],
                 row(ffn_conv_b[i]), row(g_ple[i]), row(g_final))
        weights = (w_in, w_br_a, w_br_b, w_out, w_up, w_down, w_ple_gate, w_ple_proj)
        x = _layer_call(i, i == DEPTH - 1, x, p, small, weights)
    return x
```

```python
import functools

import jax
import jax.numpy as jnp
from jax import lax
from jax.experimental import pallas as pl
from jax.experimental.pallas import tpu as pltpu

D_MODEL = 1024
DEPTH = 2
CHUNK = 64
PLE_DIM = 256
EPS = 1e-6
HG_HEADS = 4
HG_DK = 128
HG_DV = 128
HG_WIDTH = HG_HEADS * HG_DK
CONV_WIDTH = 512
CONV_K = 31
D_FF = 2816
FFN_CONV_K = 3
IN_COLS = 4 * HG_WIDTH + 2 * CONV_WIDTH + 2 * D_MODEL

OFF_Q = 0
OFF_F = HG_WIDTH
OFF_I = 2 * HG_WIDTH
OFF_OG = 3 * HG_WIDTH
OFF_GLU = 4 * HG_WIDTH
OFF_GA = OFF_GLU + 2 * CONV_WIDTH
OFF_GB = OFF_GA + D_MODEL

SUBLANES = 8
TILE_T = 256
CONV_HALO = 32
FFN_HALO = SUBLANES
FFN_COLS = 256
FILL_COLS = 256
GATE_BLOCKS_IN_CONV = 8
LANES = 128
CONV_ROWS = 128
GATE_BLOCKS = (CONV_WIDTH // LANES) * (TILE_T // CONV_ROWS)
GATE_COLS = 2 * D_MODEL // GATE_BLOCKS
LEVELS = (32, 16, 8, 4, 2, 1)
assert LEVELS[0] * 2 == CHUNK and all(a == 2 * b for a, b in zip(LEVELS, LEVELS[1:]))
STAGE_SLOTS = 3
STAGE_ROWS_WIDE = 64
STAGE_ROWS_SQ = 256
VMEM_LIMIT_BYTES = 58 * 1024 * 1024

F32 = jnp.float32
BF16 = jnp.bfloat16


def _dot(a, b):
    return jnp.dot(a, b, preferred_element_type=F32)


def _dot_nt(a, b):
    return lax.dot_general(a, b, (((1,), (1,)), ((), ())), preferred_element_type=F32)


def _dot_tn(a, b):
    return lax.dot_general(a, b, (((0,), (0,)), ((), ())), preferred_element_type=F32)


def _rms(x, g):
    return x * lax.rsqrt(jnp.mean(x * x, axis=-1, keepdims=True) + EPS) * g


def _sigmoid(x):
    return 0.5 * jnp.tanh(0.5 * x) + 0.5


def _silu(x):
    h = 0.5 * x
    return h * jnp.tanh(h) + h


def _log1p_exp_neg_abs(d):
    return jnp.log(1.0 + jnp.exp(-jnp.abs(d)))


def _log_sigmoid(z):
    return jnp.minimum(z, 0.0) - _log1p_exp_neg_abs(z)


def _split3(g):
    hi = g.astype(BF16)
    r1 = g - hi.astype(F32)
    mid = r1.astype(BF16)
    lo = (r1 - mid.astype(F32)).astype(BF16)
    return hi, mid, lo


def _level_operands(cum, q, kk, half):
    n, w = cum.shape
    if half >= SUBLANES:
        nb = n // (2 * half)
        c = cum.reshape(nb, 2, half, w)
        ref = c[:, 0, half - 1:half, :]
        q_up = q.reshape(nb, 2, half, w)[:, 1] * jnp.exp(c[:, 1] - ref)
        k_lo = kk.reshape(nb, 2, half, w)[:, 0] * jnp.exp(ref - c[:, 0])
        zero = jnp.zeros_like(q_up)
        q_l = jnp.stack([zero, q_up], axis=1).reshape(n, w)
        k_l = jnp.stack([k_lo, zero], axis=1).reshape(n, w)
        return q_l.astype(BF16), k_l.astype(BF16)
    c3 = cum.reshape(n // SUBLANES, SUBLANES, w)
    sub = lax.broadcasted_iota(jnp.int32, c3.shape, 1)
    refs = [jnp.broadcast_to(c3[:, s + half - 1:s + half, :], c3.shape)
            for s in range(0, SUBLANES, 2 * half)]
    ref = refs[-1]
    for i in range(len(refs) - 2, -1, -1):
        ref = jnp.where(sub < (i + 1) * 2 * half, refs[i], ref)
    upper = (sub & half) != 0
    e = jnp.exp(jnp.where(upper, c3 - ref, ref - c3)).reshape(n, w)
    upper = upper.reshape(n, w)
    return (jnp.where(upper, q * e, 0.0).astype(BF16),
            jnp.where(upper, 0.0, kk * e).astype(BF16))


def _chunk_tril(tt):
    ti = lax.broadcasted_iota(jnp.int32, (tt, tt), 0)
    si = lax.broadcasted_iota(jnp.int32, (tt, tt), 1)
    return ((si <= ti) & ((si ^ ti) < CHUNK)).astype(BF16)


def _hgrn_tile(zq, zf, v, zog, log_lb, log1m_lb, one_m_lb, hg_g, tril, state_ref, oa_ref,
               mxu_fill):
    tt = zq.shape[0]
    nc = tt // CHUNK

    b = log1m_lb + _log_sigmoid(zf)
    log_f = jnp.maximum(log_lb, b) + _log1p_exp_neg_abs(log_lb - b)
    kk = one_m_lb * _sigmoid(-zf)
    q = _silu(zq)

    hi, mid, lo = _split3(log_f)
    cum = _dot(tril, hi) + _dot(tril, mid) + _dot(tril, lo)

    cum4 = cum.reshape(nc, CHUNK // SUBLANES, SUBLANES, HG_WIDTH)
    last4 = cum4[:, CHUNK // SUBLANES - 1:, SUBLANES - 1:, :]
    last = jnp.broadcast_to(last4, cum4.shape).reshape(cum.shape)
    q_dec = (q * jnp.exp(cum)).astype(BF16)
    k_dec = (kk * jnp.exp(last - cum)).astype(BF16)
    chunk_decay = jnp.exp(last4.reshape(nc, HG_WIDTH))
    v_bf = v.astype(BF16)
    qk = q * kk
    gate = _silu(zog)

    ops = [_level_operands(cum, q, kk, half) for half in LEVELS]

    tc = lax.broadcasted_iota(jnp.int32, (CHUNK, CHUNK), 0)
    sc = lax.broadcasted_iota(jnp.int32, (CHUNK, CHUNK), 1)
    txs = tc ^ sc

    kv, scores = {}, {}
    for c in range(nc):
        rows = slice(c * CHUNK, (c + 1) * CHUNK)
        for hh in range(HG_HEADS):
            lanes = slice(hh * HG_DK, (hh + 1) * HG_DK)
            kv[c, hh] = _dot_tn(v_bf[rows, lanes], k_dec[rows, lanes])
            s_acc = None
            for (q_l, k_l), half in zip(ops, LEVELS):
                s_l = _dot_nt(q_l[rows, lanes], k_l[rows, lanes])
                s_acc = s_l if s_acc is None else jnp.where(txs < 2 * half, s_l, s_acc)
            scores[c, hh] = s_acc.astype(BF16)
            if hh % 2 == 1 and mxu_fill:
                mxu_fill.pop(0)()

    st_in = {}
    for hh in range(HG_HEADS):
        lanes = slice(hh * HG_DK, (hh + 1) * HG_DK)
        st = state_ref[hh]
        for c in range(nc):
            st_in[c, hh] = st.astype(BF16)
            st = st * chunk_decay[c:c + 1, lanes] + kv[c, hh]
        state_ref[hh] = st

    for c in range(nc):
        rows = slice(c * CHUNK, (c + 1) * CHUNK)
        for hh in range(HG_HEADS):
            lanes = slice(hh * HG_DK, (hh + 1) * HG_DK)
            o = _dot(scores[c, hh], v_bf[rows, lanes])
            o = o + _dot_nt(q_dec[rows, lanes], st_in[c, hh])
            o = o + jnp.sum(qk[rows, lanes], axis=-1, keepdims=True) * v[rows, lanes]
            oa_ref[rows, lanes] = (_rms(o, hg_g) * gate[rows, lanes]).astype(BF16)


def _stage_weights(layer, plan):
    per_stage = {}
    for src, dst, stage, sem in plan:
        slots, rows, width = stage.shape
        k, n = src.shape[1:]
        assert k % rows == 0 and n <= width
        ring = per_stage.setdefault(id(stage), [])
        for row0 in range(0, k, rows):
            ring.append((src, dst, stage, sem, len(ring) % slots, row0, rows, n))
    jobs = []
    rings = list(per_stage.values())
    for i in range(max(len(r) for r in rings)):
        jobs.extend(r[i] for r in rings if i < len(r))

    def job_copy(j):
        src, _, stage, sem, slot, row0, rows, n = jobs[j]
        return pltpu.make_async_copy(src.at[layer, pl.ds(row0, rows), :],
                                     stage.at[slot, :, pl.ds(0, n)], sem.at[slot])

    last_user, free_after = {}, []
    for j, (_, _, stage, _, slot, _, _, _) in enumerate(jobs):
        free_after.append(last_user.get((id(stage), slot), -1))
        last_user[id(stage), slot] = j

    started = 0
    for j in range(len(jobs)):
        while started < len(jobs) and free_after[started] < j:
            job_copy(started).start()
            started += 1
        job_copy(j).wait()
        _, dsts, stage, _, slot, row0, rows, n = jobs[j]
        assert sum(ncols for _, _, _, ncols in dsts) == n
        for ref, lead, col0, ncols in dsts:
            chunk = stage[slot, :, col0:col0 + ncols].astype(BF16)
            if lead is None:
                ref[row0:row0 + rows, :] = chunk
            else:
                ref[lead, row0:row0 + rows, :] = chunk


def _layer_kernel(layer, final_norm,
                  x_ref, p_ref, lbl_ref, g_mix_ref, hg_g_ref, b_glu_ref, cw_ref, cb_ref,
                  ln_g_ref, ln_b_ref, g_ffn_ref, fcw_ref, fcb_ref, g_ple_ref, g_fin_ref,
                  w_in_hbm, w_bra_hbm, w_brb_hbm, w_out_hbm, w_up_hbm, w_down_hbm, w_pg_hbm,
                  w_pp_hbm,
                  o_ref,
                  w_in_ref, w_bra_ref, w_brb_ref, w_out_ref, w_up_ref, w_down_ref, w_pg_ref,
                  w_pp_ref, w_gate_ref, stage_wide, stage_sq, sem_wide, sem_sq,
                  state_ref, ubuf_ref, fhalo_ref, act_ref, oa_ref, cv_ref, zg_ref, tril_ref):
    tt = TILE_T
    lane_tiles = CONV_WIDTH // LANES

    @pl.when((pl.program_id(0) == 0) & (pl.program_id(1) == 0))
    def _():
        tril_ref[...] = _chunk_tril(tt)
        wide = (stage_wide, sem_wide)
        sq = (stage_sq, sem_sq)
        whole = lambda ref: [(ref, None, 0, ref.shape[1])]
        w_in_dsts = whole(w_in_ref) + [(w_gate_ref, g, OFF_GA + g * GATE_COLS, GATE_COLS)
                                       for g in range(GATE_BLOCKS)]
        _stage_weights(layer, [
            (w_in_hbm, w_in_dsts) + wide, (w_up_hbm, whole(w_up_ref)) + wide,
            (w_bra_hbm, whole(w_bra_ref)) + sq, (w_brb_hbm, whole(w_brb_ref)) + sq,
            (w_out_hbm, whole(w_out_ref)) + sq, (w_down_hbm, whole(w_down_ref)) + sq,
            (w_pg_hbm, whole(w_pg_ref)) + sq, (w_pp_hbm, whole(w_pp_ref)) + sq])

    @pl.when(pl.program_id(1) == 0)
    def _():
        state_ref[...] = jnp.zeros_like(state_ref)
        ubuf_ref[:, 0:CONV_HALO, :] = jnp.zeros((lane_tiles, CONV_HALO, LANES), F32)
        fhalo_ref[...] = jnp.zeros_like(fhalo_ref)

    x = x_ref[0]
    h = _rms(x, g_mix_ref[...]).astype(BF16)

    lg = lbl_ref[...]
    ex = jnp.exp(lg - jnp.max(lg, axis=0, keepdims=True))
    sm = ex / jnp.sum(ex, axis=0, keepdims=True)
    cs0 = sm[0:1, :]
    cs = cs0
    for j in range(1, layer + 1):
        cs = cs + sm[j:j + 1, :]
    lb = cs - cs0
    log_lb = jnp.log(lb)
    log1m_lb = jnp.log1p(-lb)
    one_m_lb = 1.0 - lb

    zf = _dot(h, w_in_ref[:, OFF_F:OFF_F + HG_WIDTH])
    zq = _dot(h, w_in_ref[:, OFF_Q:OFF_Q + HG_WIDTH])
    zi = _dot(h, w_in_ref[:, OFF_I:OFF_I + HG_WIDTH])
    zog = _dot(h, w_in_ref[:, OFF_OG:OFF_OG + HG_WIDTH])

    zglu_blocks = []

    def glu_block(c0):
        return lambda: zglu_blocks.append(_dot(h, w_in_ref[:, c0:c0 + FILL_COLS]))

    def gate_block(g):
        def emit():
            zg_ref[g] = _dot(h, w_gate_ref[g])
        return emit

    mxu_fill = ([glu_block(c0) for c0 in range(OFF_GLU, OFF_GA, FILL_COLS)]
                + [gate_block(g) for g in range(GATE_BLOCKS - GATE_BLOCKS_IN_CONV)])
    _hgrn_tile(zq, zf, zi, zog, log_lb, log1m_lb, one_m_lb, hg_g_ref[...], tril_ref[...],
               state_ref, oa_ref, mxu_fill)
    while mxu_fill:
        mxu_fill.pop(0)()
    y_a = _dot(oa_ref[...], w_bra_ref[...])

    zglu = jnp.concatenate(zglu_blocks, axis=1) + b_glu_ref[...]
    u = zglu[:, :CONV_WIDTH] * _sigmoid(zglu[:, CONV_WIDTH:])
    for lt in range(lane_tiles):
        ubuf_ref[lt, CONV_HALO:CONV_HALO + tt, :] = u[:, lt * LANES:(lt + 1) * LANES]

    def conv_lane_tile(lt, carry):
        for blk in range(tt // CONV_ROWS):
            r0 = blk * CONV_ROWS
            acc = None
            for r in range(SUBLANES - 1, -1, -1):
                z = None
                for m in range((CONV_K - 1 - r) // SUBLANES + 1):
                    j = CONV_K - 1 - r - SUBLANES * m
                    start = CONV_HALO - SUBLANES * (m + 1) + r0
                    term = (cw_ref[lt, j:j + 1, :]
                            * ubuf_ref[lt, start:start + CONV_ROWS + SUBLANES, :])
                    z = term if z is None else z + term
                acc = z if acc is None else pltpu.roll(acc, 1, 0) + z
            cv_ref[lt, r0:r0 + CONV_ROWS, :] = acc[SUBLANES:, :] + cb_ref[lt]
            per_trip = GATE_BLOCKS_IN_CONV // lane_tiles
            if blk < per_trip:
                g = GATE_BLOCKS - GATE_BLOCKS_IN_CONV + lt * per_trip + blk
                zg_ref[g] = _dot(h, w_gate_ref[g])
        return carry

    lax.fori_loop(0, lane_tiles, conv_lane_tile, 0)
    zga = jnp.concatenate([zg_ref[g] for g in range(GATE_BLOCKS // 2)], axis=1)
    zgb = jnp.concatenate([zg_ref[g] for g in range(GATE_BLOCKS // 2, GATE_BLOCKS)], axis=1)
    acc = jnp.concatenate([cv_ref[lt] for lt in range(lane_tiles)], axis=1)
    for lt in range(lane_tiles):
        ubuf_ref[lt, 0:CONV_HALO, :] = ubuf_ref[lt, tt:tt + CONV_HALO, :]
    mu = jnp.mean(acc, axis=-1, keepdims=True)
    d = acc - mu
    var = jnp.mean(d * d, axis=-1, keepdims=True)
    ub = _silu(d * lax.rsqrt(var + EPS) * ln_g_ref[...] + ln_b_ref[...])
    y_b = _dot(ub.astype(BF16), w_brb_ref[...])

    y = _sigmoid(zga) * y_a + _sigmoid(zgb) * y_b
    x = x + _dot(y.astype(BF16), w_out_ref[...])

    hf = _rms(x, g_ffn_ref[...]).astype(BF16)

    def ffn_conv(col0):
        cols = slice(col0, col0 + FFN_COLS)
        up = _dot(hf, w_up_ref[:, cols])
        ext = jnp.concatenate([fhalo_ref[:, cols], up], axis=0)
        fhalo_ref[:, cols] = up[tt - FFN_HALO:, :]
        s = fcw_ref[0:1, cols] * ext
        s = pltpu.roll(s, 1, 0) + fcw_ref[1:2, cols] * ext
        s = pltpu.roll(s, 1, 0) + fcw_ref[2:3, cols] * ext
        return s[FFN_HALO:, :] + fcb_ref[:, cols]

    for j in range(D_FF // FFN_COLS):
        ca = ffn_conv(j * FFN_COLS)
        cg = ffn_conv(D_FF + j * FFN_COLS)
        act_ref[:, j * FFN_COLS:(j + 1) * FFN_COLS] = (_silu(ca) * cg).astype(BF16)
    x = x + _dot(act_ref[...], w_down_ref[...])

    hp = _rms(x, g_ple_ref[...]).astype(BF16)
    gate = _sigmoid(_dot(hp, w_pg_ref[...]))
    x = x + gate * _dot(p_ref[0, 0].astype(BF16), w_pp_ref[...])

    if final_norm:
        x = _rms(x, g_fin_ref[...])
    o_ref[0] = x


def _resident(shape):
    nd = len(shape)
    return pl.BlockSpec(shape, lambda b, t: (0,) * nd, pipeline_mode=pl.Buffered(1))


def _layer_call(layer, final_norm, x, p, small, weights):
    bsz, seq, _ = x.shape
    tt = TILE_T
    assert seq % tt == 0 and tt % CHUNK == 0
    operands = (x, p) + tuple(small) + tuple(weights)
    in_specs = (
        [pl.BlockSpec((1, tt, D_MODEL), lambda b, t: (b, t, 0)),
         pl.BlockSpec((1, 1, tt, PLE_DIM), lambda b, t: (layer, b, t, 0))]
        + [_resident(a.shape) for a in small]
        + [pl.BlockSpec(memory_space=pl.ANY) for _ in weights])
    w_in = weights[0]
    resident = [(w_in.shape[1], OFF_GA)] + [w.shape[1:] for w in weights[1:]]
    scratch = (
        [pltpu.VMEM(shape, BF16) for shape in resident]
        + [pltpu.VMEM((GATE_BLOCKS, D_MODEL, GATE_COLS), BF16),
           pltpu.VMEM((STAGE_SLOTS, STAGE_ROWS_WIDE, max(IN_COLS, 2 * D_FF)), F32),
           pltpu.VMEM((STAGE_SLOTS, STAGE_ROWS_SQ, D_MODEL), F32),
           pltpu.SemaphoreType.DMA((STAGE_SLOTS,)),
           pltpu.SemaphoreType.DMA((STAGE_SLOTS,))])
    scratch += [
        pltpu.VMEM((HG_HEADS, HG_DV, HG_DK), F32),
        pltpu.VMEM((CONV_WIDTH // LANES, CONV_HALO + tt, LANES), F32),
        pltpu.VMEM((FFN_HALO, 2 * D_FF), F32),
        pltpu.VMEM((tt, D_FF), BF16),
        pltpu.VMEM((tt, HG_WIDTH), BF16),
        pltpu.VMEM((CONV_WIDTH // LANES, tt, LANES), F32),
        pltpu.VMEM((GATE_BLOCKS, tt, GATE_COLS), F32),
        pltpu.VMEM((tt, tt), BF16),
    ]
    return pl.pallas_call(
        functools.partial(_layer_kernel, layer, final_norm),
        grid=(bsz, seq // tt),
        in_specs=in_specs,
        out_specs=pl.BlockSpec((1, tt, D_MODEL), lambda b, t: (b, t, 0)),
        out_shape=jax.ShapeDtypeStruct(x.shape, F32),
        scratch_shapes=scratch,
        compiler_params=pltpu.CompilerParams(
            dimension_semantics=("arbitrary", "arbitrary"),
            vmem_limit_bytes=VMEM_LIMIT_BYTES),
        name=f"trunk_layer{layer}",
    )(*operands)


def kernel(x, p, g_mix, w_in, hg_lb_logits, hg_norm_g, w_br_a, b_glu, conv_w, conv_b, ln_g, ln_b,
           w_br_b, w_out, g_ffn, w_up, ffn_conv_w, ffn_conv_b, w_down, g_ple, w_ple_gate,
           w_ple_proj, g_final):
    row = lambda a: a.reshape(1, -1)
    lane_tiled = lambda a: a.reshape(a.shape[0], CONV_WIDTH // LANES, LANES).transpose(1, 0, 2)
    for i in range(DEPTH):
        small = (hg_lb_logits, row(g_mix[i]), row(hg_norm_g[i]), row(b_glu[i]),
                 lane_tiled(conv_w[i]),
                 lane_tiled(row(conv_b[i])), row(ln_g[i]), row(ln_b[i]), row(g_ffn[i]), ffn_conv_w[i],
                 row(ffn_conv_b[i]), row(g_ple[i]), row(g_final))
        weights = (w_in, w_br_a, w_br_b, w_out, w_up, w_down, w_ple_gate, w_ple_proj)
        x = _layer_call(i, i == DEPTH - 1, x, p, small, weights)
    return x
```
